```python
import jax, jax.numpy as jnp
from jax import lax
import numpy as np

D_MODEL = 1024
BATCH = 8
SEQ = 4096
DEPTH = 4

N_MIXERS = 2
N_A = (DEPTH + 1) // 2
N_B = DEPTH // 2

D_RNN = 3 * D_MODEL // 2
LRU_HEADS = 12
LRU_BW = D_RNN // LRU_HEADS
CONV_WIDTH = 4
LRU_C = 8.0

D_POOL = D_MODEL
POOL_WINDOWS = (2, 4, 8, 16)
POOL_GROUPS = len(POOL_WINDOWS)
POOL_GW = D_POOL // POOL_GROUPS

D_FF = 4 * D_MODEL
PLE_DIM = 256
ALPHA = (2 * DEPTH) ** 0.25
BETA = (8 * DEPTH) ** (-0.25)
LN_EPS = 1e-5

kernel_name = "hybrid_rglru_pool_deepnorm_trunk"


def layer_norm(x, g, b):
    xf = x.astype(jnp.float32)
    mu = jnp.mean(xf, axis=-1, keepdims=True)
    var = jnp.mean(jnp.square(xf - mu), axis=-1, keepdims=True)
    y = (xf - mu) * lax.rsqrt(var + LN_EPS)
    return (y * g.astype(jnp.float32) + b.astype(jnp.float32)).astype(x.dtype)


def causal_depthwise_conv(u, w, b):
    s = u.shape[1]
    up = jnp.pad(u, ((0, 0), (CONV_WIDTH - 1, 0), (0, 0)))
    out = b
    for k in range(CONV_WIDTH):
        out = out + up[:, k:k + s] * w[k]
    return out


def _lin_rec_combine(left, right):
    a1, b1 = left
    a2, b2 = right
    return a1 * a2, a2 * b1 + b2


def rg_lru(u, wa, ba, wx, bx, lam):
    bsz, s, _ = u.shape
    uh = u.reshape(bsz, s, LRU_HEADS, LRU_BW)
    r = jax.nn.sigmoid(jnp.einsum('bshi,hij->bshj', uh, wa).reshape(bsz, s, D_RNN) + ba)
    ig = jax.nn.sigmoid(jnp.einsum('bshi,hij->bshj', uh, wx).reshape(bsz, s, D_RNN) + bx)
    log_a = -LRU_C * r.astype(jnp.float32) * jax.nn.softplus(-lam.astype(jnp.float32))
    a = jnp.exp(log_a)
    mult = jnp.sqrt(-jnp.expm1(2.0 * log_a))
    mult = mult.at[:, 0].set(1.0)
    bterm = mult * (ig * u).astype(jnp.float32)
    _, h = lax.associative_scan(_lin_rec_combine, (a, bterm), axis=1)
    return h.astype(u.dtype)


def recurrent_mixer(x, w_in, conv_w, conv_b, wa, ba, wx, bx, lam, w_out):
    proj = x @ w_in
    u, y = proj[..., :D_RNN], proj[..., D_RNN:]
    u = causal_depthwise_conv(u, conv_w, conv_b)
    h = rg_lru(u, wa, ba, wx, bx, lam)
    return (h * jax.nn.gelu(y)) @ w_out


def pooling_mixer(x, w_in, w_grp, b_grp, scale, w_out):
    u = x @ w_in
    s = u.shape[1]
    pos = jnp.arange(s, dtype=jnp.int32)
    outs = []
    for g, w in enumerate(POOL_WINDOWS):
        ug = u[..., g * POOL_GW:(g + 1) * POOL_GW].astype(jnp.float32)
        cs = jnp.cumsum(ug, axis=1)
        cs_prev = jnp.pad(cs, ((0, 0), (w, 0), (0, 0)))[:, :s]
        cnt = jnp.minimum(pos + 1, w).astype(jnp.float32)[None, :, None]
        outs.append((cs - cs_prev) / cnt - ug)
    z = jnp.stack(outs, axis=2).astype(u.dtype)
    z = jnp.einsum('bsgi,gij->bsgj', z, w_grp).reshape(u.shape) + b_grp
    return (z * scale) @ w_out


def sq_relu_mlp(x, w1, w2):
    h = jax.nn.relu(x @ w1)
    return (h * h) @ w2


def setup_inputs(seed: int = 0) -> dict:
    key = jax.random.key(seed)
    ks = jax.random.split(key, 26)
    f32 = jnp.float32

    def nrm(k, shape, scale):
        return jax.random.normal(k, shape, f32) * scale

    a_c = jax.random.uniform(ks[8], (N_A, D_RNN), f32, minval=0.9, maxval=0.999)
    a0 = a_c ** (1.0 / LRU_C)
    lam = jnp.log(a0) - jnp.log1p(-a0)
    return {
        "x": nrm(ks[0], (BATCH, SEQ, D_MODEL), 1.0),
        "p": nrm(ks[1], (DEPTH, BATCH, SEQ, PLE_DIM), 1.0),
        "lru_w_in": nrm(ks[2], (N_A, D_MODEL, 2 * D_RNN), D_MODEL ** -0.5),
        "lru_conv_w": nrm(ks[3], (N_A, CONV_WIDTH, D_RNN), CONV_WIDTH ** -0.5),
        "lru_conv_b": nrm(ks[4], (N_A, D_RNN), 0.01),
        "lru_wa": nrm(ks[5], (N_A, LRU_HEADS, LRU_BW, LRU_BW), LRU_BW ** -0.5),
        "lru_ba": nrm(ks[6], (N_A, D_RNN), 0.01),
        "lru_wx": nrm(ks[7], (N_A, LRU_HEADS, LRU_BW, LRU_BW), LRU_BW ** -0.5),
        "lru_bx": nrm(ks[9], (N_A, D_RNN), 0.01),
        "lru_lambda": lam,
        "lru_w_out": nrm(ks[10], (N_A, D_RNN, D_MODEL), BETA * D_RNN ** -0.5),
        "pool_w_in": nrm(ks[11], (N_B, D_MODEL, D_POOL), D_MODEL ** -0.5),
        "pool_w_grp": nrm(ks[12], (N_B, POOL_GROUPS, POOL_GW, POOL_GW), POOL_GW ** -0.5),
        "pool_b_grp": nrm(ks[13], (N_B, D_POOL), 0.01),
        "pool_scale": 1.0 + nrm(ks[14], (N_B, D_POOL), 0.1),
        "pool_w_out": nrm(ks[15], (N_B, D_POOL, D_MODEL), BETA * D_POOL ** -0.5),
        "ln_mix_g": 1.0 + nrm(ks[16], (DEPTH, D_MODEL), 0.05),
        "ln_mix_b": nrm(ks[17], (DEPTH, D_MODEL), 0.01),
        "mlp_w1": nrm(ks[18], (DEPTH, D_MODEL, D_FF), D_MODEL ** -0.5),
        "mlp_w2": nrm(ks[19], (DEPTH, D_FF, D_MODEL), BETA * D_FF ** -0.5),
        "ln_mlp_g": 1.0 + nrm(ks[20], (DEPTH, D_MODEL), 0.05),
        "ln_mlp_b": nrm(ks[21], (DEPTH, D_MODEL), 0.01),
        "ple_w": nrm(ks[22], (DEPTH, PLE_DIM, D_MODEL), PLE_DIM ** -0.5),
        "ple_gate_w": nrm(ks[23], (DEPTH, D_MODEL, D_MODEL), D_MODEL ** -0.5),
        "ple_gate_b": nrm(ks[24], (DEPTH, D_MODEL), 0.01),
    }


def reference(x, p, lru_w_in, lru_conv_w, lru_conv_b, lru_wa, lru_ba, lru_wx, lru_bx,
              lru_lambda, lru_w_out, pool_w_in, pool_w_grp, pool_b_grp, pool_scale,
              pool_w_out, ln_mix_g, ln_mix_b, mlp_w1, mlp_w2, ln_mlp_g, ln_mlp_b,
              ple_w, ple_gate_w, ple_gate_b):
    for i in range(DEPTH):
        slot = i // N_MIXERS
        if i % N_MIXERS == 0:
            m = recurrent_mixer(x, lru_w_in[slot], lru_conv_w[slot], lru_conv_b[slot],
                                lru_wa[slot], lru_ba[slot], lru_wx[slot], lru_bx[slot],
                                lru_lambda[slot], lru_w_out[slot])
        else:
            m = pooling_mixer(x, pool_w_in[slot], pool_w_grp[slot], pool_b_grp[slot],
                              pool_scale[slot], pool_w_out[slot])
        x = layer_norm(ALPHA * x + m, ln_mix_g[i], ln_mix_b[i])
        x = layer_norm(ALPHA * x + sq_relu_mlp(x, mlp_w1[i], mlp_w2[i]), ln_mlp_g[i], ln_mlp_b[i])
        gate = jax.nn.sigmoid(x @ ple_gate_w[i] + ple_gate_b[i])
        x = x + (p[i] @ ple_w[i]) * gate
    return x
```

```python
import functools

import jax
import jax.numpy as jnp
from jax import lax
from jax.experimental import pallas as pl
from jax.experimental.pallas import tpu as pltpu

_LRU_C = 8.0
_CONV_WIDTH = 4
_POOL_WINDOWS = (2, 4, 8, 16)
_LN_EPS = 1e-5

_F32 = jnp.float32
_BF16 = jnp.bfloat16

_V7X_VMEM_BYTES = 64 * 1024 * 1024
_VMEM_LIMIT_BYTES = _V7X_VMEM_BYTES - 8 * 1024 * 1024

_MIXER_TIME_BLOCK = 32
_MLP_ROW_BLOCK = 512


def _layer_norm(z, g, b):
    mu = jnp.mean(z, axis=-1, keepdims=True)
    zc = z - mu
    var = jnp.mean(zc * zc, axis=-1, keepdims=True)
    return zc * lax.rsqrt(var + _LN_EPS) * g + b


def _dot(a, w):
    return jnp.dot(a.astype(_BF16), w, preferred_element_type=_F32)


def _lru_mixer_kernel(x_ref, w_in_ref, cw_ref, cb_ref, wg_ref, bg_ref, lam_ref, w_out_ref,
                      g_ref, b_ref, o_ref, ubuf_ref, a_ref, bt_ref, h_ref,
                      *, tb, nb, heads, bw, alpha):
    rows = tb * nb
    hist = (_CONV_WIDTH - 1) * nb
    d_rnn = heads * bw
    step = pl.program_id(0)

    @pl.when(step == 0)
    def _():
        ubuf_ref[0:hist, :] = jnp.zeros((hist, d_rnn), _F32)
        h_ref[...] = jnp.zeros((nb, d_rnn), _F32)

    x = x_ref[...]
    proj = _dot(x, w_in_ref[...])
    ubuf_ref[hist:hist + rows, :] = proj[:, :d_rnn]
    y = proj[:, d_rnn:]

    sp = jax.nn.softplus(-lam_ref[...])
    row_id = lax.broadcasted_iota(jnp.int32, (rows, bw), 0)
    seq_start = jnp.logical_and(step == 0, row_id < nb)

    for h in range(heads):
        cs = slice(h * bw, (h + 1) * bw)
        uc = jnp.broadcast_to(cb_ref[:, cs], (rows, bw))
        for k in range(_CONV_WIDTH):
            uc = uc + ubuf_ref[k * nb:k * nb + rows, cs] * cw_ref[k:k + 1, cs]
        gates = _dot(uc, wg_ref[h]) + bg_ref[h]
        r = jax.nn.sigmoid(gates[:, :bw])
        ig = jax.nn.sigmoid(gates[:, bw:])
        log_a = (-_LRU_C * r) * sp[:, cs]
        a = jnp.exp(log_a)
        mult = jnp.sqrt(1.0 - a * a)
        mult = jnp.where(seq_start, 1.0, mult)
        a_ref[:, cs] = a
        bt_ref[:, cs] = mult * (ig * uc)

    ubuf_ref[0:hist, :] = ubuf_ref[rows:rows + hist, :]

    def scan_step(i, hprev):
        r0 = pl.multiple_of(i * nb, nb)
        hnew = a_ref[pl.ds(r0, nb), :] * hprev + bt_ref[pl.ds(r0, nb), :]
        bt_ref[pl.ds(r0, nb), :] = hnew
        return hnew

    h_ref[...] = lax.fori_loop(0, tb, scan_step, h_ref[...], unroll=8)

    m = _dot(bt_ref[...] * jax.nn.gelu(y), w_out_ref[...])
    o_ref[...] = _layer_norm(alpha * x + m, g_ref[...], b_ref[...])


def _pool_mixer_kernel(x_ref, w_in_ref, wgrp_ref, bgrp_ref, scale_ref, w_out_ref,
                       g_ref, b_ref, o_ref, ubuf_ref, *, tb, nb, gw, alpha):
    rows = tb * nb
    hist = (max(_POOL_WINDOWS) - 1) * nb
    d_pool = gw * len(_POOL_WINDOWS)
    step = pl.program_id(0)

    @pl.when(step == 0)
    def _():
        ubuf_ref[0:hist, :] = jnp.zeros((hist, d_pool), _F32)

    x = x_ref[...]
    ubuf_ref[hist:hist + rows, :] = _dot(x, w_in_ref[...])

    t_idx = step * tb + lax.shift_right_logical(
        lax.broadcasted_iota(jnp.int32, (rows, gw), 0), nb.bit_length() - 1)

    mixed = []
    for g, w in enumerate(_POOL_WINDOWS):
        cs = slice(g * gw, (g + 1) * gw)
        s = ubuf_ref[hist - (w - 1) * nb:hist + rows, cs]
        d = 1
        while d < w:
            n = s.shape[0]
            s = s[d * nb:, :] + s[:n - d * nb, :]
            d *= 2
        cnt = jnp.minimum(t_idx + 1, w).astype(_F32)
        z = s / cnt - ubuf_ref[hist:hist + rows, cs]
        zg = _dot(z, wgrp_ref[g]) + bgrp_ref[:, cs]
        mixed.append(zg * scale_ref[:, cs])

    ubuf_ref[0:hist, :] = ubuf_ref[rows:rows + hist, :]

    m = _dot(jnp.concatenate(mixed, axis=1), w_out_ref[...])
    o_ref[...] = _layer_norm(alpha * x + m, g_ref[...], b_ref[...])


def _mlp_ple_kernel(x_ref, p_ref, w1_ref, w2_ref, g_ref, b_ref, wple_ref, wgate_ref, bgate_ref,
                    o_ref, *, alpha):
    x = x_ref[...]
    h = jnp.maximum(_dot(x, w1_ref[...]), 0.0)
    m = _dot(h * h, w2_ref[...])
    x2 = _layer_norm(alpha * x + m, g_ref[...], b_ref[...])
    gate = jax.nn.sigmoid(_dot(x2, wgate_ref[...]) + bgate_ref[...])
    o_ref[...] = x2 + _dot(p_ref[...], wple_ref[...]) * gate


def _resident(shape, slot):
    nd = len(shape)
    return pl.BlockSpec((None,) + tuple(shape[1:]), lambda j: (slot,) + (0,) * (nd - 1),
                        pipeline_mode=pl.Buffered(1))


def _rows_spec(rows, width):
    return pl.BlockSpec((rows, width), lambda j: (j, 0))


def _lru_mixer(x2d, slot, layer, w_in, cw, cb, wg, bg, lam, w_out, ln_g, ln_b, *, nb, alpha):
    n_rows, d_model = x2d.shape
    heads, bw = wg.shape[1], wg.shape[2]
    d_rnn = heads * bw
    tb = _MIXER_TIME_BLOCK
    rows = tb * nb
    hist = (_CONV_WIDTH - 1) * nb
    kern = functools.partial(_lru_mixer_kernel, tb=tb, nb=nb, heads=heads, bw=bw, alpha=alpha)
    return pl.pallas_call(
        kern,
        grid=(n_rows // rows,),
        in_specs=[
            _rows_spec(rows, d_model),
            _resident(w_in.shape, slot), _resident(cw.shape, slot), _resident(cb.shape, slot),
            _resident(wg.shape, slot), _resident(bg.shape, slot), _resident(lam.shape, slot),
            _resident(w_out.shape, slot), _resident(ln_g.shape, layer), _resident(ln_b.shape, layer),
        ],
        out_specs=_rows_spec(rows, d_model),
        out_shape=jax.ShapeDtypeStruct(x2d.shape, _F32),
        scratch_shapes=[
            pltpu.VMEM((hist + rows, d_rnn), _F32),
            pltpu.VMEM((rows, d_rnn), _F32),
            pltpu.VMEM((rows, d_rnn), _F32),
            pltpu.VMEM((nb, d_rnn), _F32),
        ],
        compiler_params=pltpu.CompilerParams(
            dimension_semantics=("arbitrary",), vmem_limit_bytes=_VMEM_LIMIT_BYTES),
        name="lru_mixer",
    )(x2d, w_in, cw, cb, wg, bg, lam, w_out, ln_g, ln_b)


def _pool_mixer(x2d, slot, layer, w_in, wgrp, bgrp, scale, w_out, ln_g, ln_b, *, nb, alpha):
    n_rows, d_model = x2d.shape
    gw = wgrp.shape[2]
    d_pool = gw * len(_POOL_WINDOWS)
    tb = _MIXER_TIME_BLOCK
    rows = tb * nb
    hist = (max(_POOL_WINDOWS) - 1) * nb
    assert rows >= hist and nb & (nb - 1) == 0
    kern = functools.partial(_pool_mixer_kernel, tb=tb, nb=nb, gw=gw, alpha=alpha)
    return pl.pallas_call(
        kern,
        grid=(n_rows // rows,),
        in_specs=[
            _rows_spec(rows, d_model),
            _resident(w_in.shape, slot), _resident(wgrp.shape, slot), _resident(bgrp.shape, slot),
            _resident(scale.shape, slot), _resident(w_out.shape, slot),
            _resident(ln_g.shape, layer), _resident(ln_b.shape, layer),
        ],
        out_specs=_rows_spec(rows, d_model),
        out_shape=jax.ShapeDtypeStruct(x2d.shape, _F32),
        scratch_shapes=[pltpu.VMEM((hist + rows, d_pool), _F32)],
        compiler_params=pltpu.CompilerParams(
            dimension_semantics=("arbitrary",), vmem_limit_bytes=_VMEM_LIMIT_BYTES),
        name="pool_mixer",
    )(x2d, w_in, wgrp, bgrp, scale, w_out, ln_g, ln_b)


def _mlp_ple(x2d, p3d, layer, w1, w2, ln_g, ln_b, wple, wgate, bgate, *, alpha):
    n_rows, d_model = x2d.shape
    ple_dim = p3d.shape[2]
    rows = _MLP_ROW_BLOCK
    kern = functools.partial(_mlp_ple_kernel, alpha=alpha)
    return pl.pallas_call(
        kern,
        grid=(n_rows // rows,),
        in_specs=[
            _rows_spec(rows, d_model),
            pl.BlockSpec((None, rows, ple_dim), lambda j: (layer, j, 0)),
            _resident(w1.shape, layer), _resident(w2.shape, layer),
            _resident(ln_g.shape, layer), _resident(ln_b.shape, layer),
            _resident(wple.shape, layer), _resident(wgate.shape, layer), _resident(bgate.shape, layer),
        ],
        out_specs=_rows_spec(rows, d_model),
        out_shape=jax.ShapeDtypeStruct(x2d.shape, _F32),
        compiler_params=pltpu.CompilerParams(
            dimension_semantics=("parallel",), vmem_limit_bytes=_VMEM_LIMIT_BYTES),
        name="mlp_ple",
    )(x2d, p3d, w1, w2, ln_g, ln_b, wple, wgate, bgate)


def kernel(x, p, lru_w_in, lru_conv_w, lru_conv_b, lru_wa, lru_ba, lru_wx, lru_bx, lru_lambda, lru_w_out, pool_w_in, pool_w_grp, pool_b_grp, pool_scale, pool_w_out, ln_mix_g, ln_mix_b, mlp_w1, mlp_w2, ln_mlp_g, ln_mlp_b, ple_w, ple_gate_w, ple_gate_b):
    nb, seq, d_model = x.shape
    depth = p.shape[0]
    n_a, heads, bw, _ = lru_wa.shape
    alpha = float((2 * depth) ** 0.25)

    x2d = x.transpose(1, 0, 2).reshape(seq * nb, d_model)
    p3d = p.transpose(0, 2, 1, 3).reshape(depth, seq * nb, p.shape[-1])

    row = lambda v: v[:, None, :]
    bf = lambda w: w.astype(_BF16)
    wg = bf(jnp.concatenate([lru_wa, lru_wx], axis=-1))
    bg = jnp.concatenate([lru_ba.reshape(n_a, heads, 1, bw), lru_bx.reshape(n_a, heads, 1, bw)], axis=-1)
    lru_w_in, lru_w_out = bf(lru_w_in), bf(lru_w_out)
    pool_w_in, pool_w_grp, pool_w_out = bf(pool_w_in), bf(pool_w_grp), bf(pool_w_out)
    mlp_w1, mlp_w2 = bf(mlp_w1), bf(mlp_w2)
    ple_w, ple_gate_w = bf(ple_w), bf(ple_gate_w)
    ln_mix_g, ln_mix_b, ln_mlp_g, ln_mlp_b = row(ln_mix_g), row(ln_mix_b), row(ln_mlp_g), row(ln_mlp_b)
    ple_gate_b = row(ple_gate_b)
    lru_conv_b, lru_lambda = row(lru_conv_b), row(lru_lambda)
    pool_b_grp, pool_scale = row(pool_b_grp), row(pool_scale)

    for i in range(depth):
        slot = i // 2
        if i % 2 == 0:
            x2d = _lru_mixer(x2d, slot, i, lru_w_in, lru_conv_w, lru_conv_b, wg, bg, lru_lambda,
                             lru_w_out, ln_mix_g, ln_mix_b, nb=nb, alpha=alpha)
        else:
            x2d = _pool_mixer(x2d, slot, i, pool_w_in, pool_w_grp, pool_b_grp, pool_scale,
                              pool_w_out, ln_mix_g, ln_mix_b, nb=nb, alpha=alpha)
        x2d = _mlp_ple(x2d, p3d, i, mlp_w1, mlp_w2, ln_mlp_g, ln_mlp_b, ple_w, ple_gate_w,
                       ple_gate_b, alpha=alpha)

    return x2d.reshape(seq, nb, d_model).transpose(1, 0, 2)
```

```python
import functools
import math

import jax
import jax.numpy as jnp
from jax import lax
from jax.experimental import pallas as pl
from jax.experimental.pallas import tpu as pltpu

_LRU_C = 8.0
_CONV_WIDTH = 4
_POOL_WINDOWS = (2, 4, 8, 16)
_LN_EPS = 1e-5

_LOG2_E = math.log2(math.e)
_GELU_C0 = math.sqrt(2.0 / math.pi)
_GELU_C1 = 0.044715 * _GELU_C0
_SQRT_FLOOR = 1e-30

_F32 = jnp.float32
_BF16 = jnp.bfloat16

_V7X_VMEM_BYTES = 64 * 1024 * 1024
_V7X_LANES = 128
_VMEM_LIMIT_BYTES = _V7X_VMEM_BYTES - 8 * 1024 * 1024

_MIXER_TIME_BLOCK = 32
_LRU_OUT_CHUNK_HEADS = 3
_LRU_CHUNK_STEPS = 8
_MLP_ROW_BLOCK = 512


def _layer_norm(z, g, b):
    mu = jnp.mean(z, axis=-1, keepdims=True)
    zc = z - mu
    var = jnp.mean(zc * zc, axis=-1, keepdims=True)
    return zc * lax.rsqrt(var + _LN_EPS) * g + b


def _dot(a, w):
    return jnp.dot(a.astype(_BF16), w, preferred_element_type=_F32)


def _sigmoid(v):
    return 0.5 * jnp.tanh(0.5 * v) + 0.5


def _gelu_tanh(v):
    inner = v * (_GELU_C0 + _GELU_C1 * (v * v))
    return (0.5 * v) * (1.0 + jnp.tanh(inner))


def _lru_stages(step, x_ref, xres_ref, w_in_ref, cw_ref, cb_ref, wg_ref, bg_ref, lam_ref, w_out_ref,
                g_ref, b_ref, o_ref, u_new, y_new, u_cur, y_cur, hg_new, hg_old, z_new, z_old,
                uc_buf, gate_buf, h_ref, *, tb, nb, heads, bw, alpha):
    rows = tb * nb
    hist = (_CONV_WIDTH - 1) * nb
    d_model = o_ref.shape[1]
    out_chunk = d_model * _LRU_OUT_CHUNK_HEADS // heads
    ln_heads = [h for h in range(heads) if h % _LRU_OUT_CHUNK_HEADS != _LRU_OUT_CHUNK_HEADS - 1]
    ln_rows = rows // len(ln_heads)
    chunk = _LRU_CHUNK_STEPS * nb

    u_new[0:hist, :] = u_cur[rows:rows + hist, :]

    xb = x_ref[...].astype(_BF16)
    decay = (-_LRU_C * _LOG2_E) * jax.nn.softplus(-lam_ref[...])
    first_block = step == 1
    def conv_and_gates(h):
        cs = slice(h * bw, (h + 1) * bw)
        uc = jnp.broadcast_to(cb_ref[:, cs], (rows, bw))
        for k in range(_CONV_WIDTH):
            uc = uc + u_cur[k * nb:k * nb + rows, cs] * cw_ref[k:k + 1, cs]
        uc_buf[h % 2][...] = uc
        gate_buf[h % 2][...] = _dot(uc, wg_ref[h]) + bg_ref[h]

    conv_and_gates(0)
    for h in range(heads):
        cs = slice(h * bw, (h + 1) * bw)
        proj = jnp.dot(xb, w_in_ref[:, 2 * h * bw:2 * (h + 1) * bw], preferred_element_type=_F32)
        u_new[hist:hist + rows, cs] = proj[:, :bw]
        y_new[:, cs] = proj[:, bw:]

        if h + 1 < heads:
            conv_and_gates(h + 1)

        if h % _LRU_OUT_CHUNK_HEADS == _LRU_OUT_CHUNK_HEADS - 1:
            oc = slice(h // _LRU_OUT_CHUNK_HEADS * out_chunk, (h // _LRU_OUT_CHUNK_HEADS + 1) * out_chunk)
            m = jnp.dot(hg_old[...], w_out_ref[:, oc], preferred_element_type=_F32)
            z_new[:, oc] = alpha * xres_ref[:, oc] + m
        else:
            rs = slice(ln_heads.index(h) * ln_rows, (ln_heads.index(h) + 1) * ln_rows)
            o_ref[rs, :] = _layer_norm(z_old[rs, :], g_ref[...], b_ref[...])

        hprev = jnp.where(step <= 1, 0.0, h_ref[:, cs])
        for c in range(rows // chunk):
            rsl = slice(c * chunk, (c + 1) * chunk)
            gates = gate_buf[h % 2][rsl, :]
            r = _sigmoid(gates[:, :bw])
            gated = _sigmoid(gates[:, bw:]) * uc_buf[h % 2][rsl, :]
            a = jnp.exp2(r * decay[:, cs])
            v = 1.0 - a * a
            bterm = (v * lax.rsqrt(jnp.maximum(v, _SQRT_FLOOR))) * gated
            hs = []
            for t in range(chunk // nb):
                sl = slice(t * nb, (t + 1) * nb)
                b_t = bterm[sl]
                if c == 0 and t == 0:
                    b_t = jnp.where(first_block, gated[sl], b_t)
                hprev = a[sl] * hprev + b_t
                hs.append(hprev)
            hg = jnp.concatenate(hs, axis=0) * _gelu_tanh(y_cur[rsl, cs])
            hg_new[rsl, cs] = hg.astype(_BF16)
        h_ref[:, cs] = hprev


def _lru_mixer_kernel(x_ref, xres_ref, w_in_ref, cw_ref, cb_ref, wg_ref, bg_ref, lam_ref, w_out_ref,
                      g_ref, b_ref, o_ref, u0, u1, y0, y1, hg0, hg1, z0, z1, uc0, uc1, gt0, gt1, h_ref,
                      **static):
    step = pl.program_id(0)

    @pl.when(step == 0)
    def _():
        for ref in (u0, u1, y0, y1, hg0, hg1, z0, z1, h_ref):
            ref[...] = jnp.zeros(ref.shape, ref.dtype)

    stages = functools.partial(_lru_stages, step, x_ref, xres_ref, w_in_ref, cw_ref, cb_ref, wg_ref,
                               bg_ref, lam_ref, w_out_ref, g_ref, b_ref, o_ref)
    parity = lax.rem(step, 2)

    @pl.when(parity == 0)
    def _():
        stages(u0, y0, u1, y1, hg1, hg0, z0, z1, (uc0, uc1), (gt0, gt1), h_ref, **static)

    @pl.when(parity == 1)
    def _():
        stages(u1, y1, u0, y0, hg0, hg1, z1, z0, (uc0, uc1), (gt0, gt1), h_ref, **static)


def _pool_stages(step, x_ref, xres_ref, w_in_ref, wgrp_ref, bgrp_ref, scale_ref, w_out_ref,
                 g_ref, b_ref, o_ref, u_new, u_cur, z_new, z_old, grp_buf, zc_buf,
                 *, tb, nb, gw, alpha):
    rows = tb * nb
    lanes = _V7X_LANES
    hist = (max(_POOL_WINDOWS) - 1) * nb
    groups = len(_POOL_WINDOWS)

    u_new[0:hist, :] = u_cur[rows:rows + hist, :]

    xb = x_ref[...].astype(_BF16)
    t_idx = (step - 1) * tb + lax.shift_right_logical(
        lax.broadcasted_iota(jnp.int32, (rows, lanes), 0), nb.bit_length() - 1)
    t_idx = jnp.maximum(t_idx, 0)

    def pool_and_group(g):
        w = _POOL_WINDOWS[g]
        cs = slice(g * gw, (g + 1) * gw)
        s = u_cur[hist - (w - 1) * nb:hist + rows, cs]
        d = 1
        while d < w:
            n = s.shape[0]
            s = s[d * nb:, :] + s[:n - d * nb, :]
            d *= 2
        inv_cnt = 1.0 / jnp.minimum(t_idx + 1, w).astype(_F32)
        inv_cnt = jnp.concatenate([inv_cnt] * (gw // lanes), axis=1)
        z = s * inv_cnt - u_cur[hist:hist + rows, cs]
        grp_buf[g % 2][...] = _dot(z, wgrp_ref[g])

    pool_and_group(0)
    for g in range(groups):
        cs = slice(g * gw, (g + 1) * gw)
        u_new[hist:hist + rows, cs] = jnp.dot(xb, w_in_ref[:, cs], preferred_element_type=_F32)
        if g + 1 < groups:
            pool_and_group(g + 1)
        zc_buf[:, cs] = ((grp_buf[g % 2][...] + bgrp_ref[:, cs]) * scale_ref[:, cs]).astype(_BF16)

    m = jnp.dot(zc_buf[...], w_out_ref[...], preferred_element_type=_F32)
    ln_rows = rows // groups
    for c in range(groups):
        rs = slice(c * ln_rows, (c + 1) * ln_rows)
        o_ref[rs, :] = _layer_norm(z_old[rs, :], g_ref[...], b_ref[...])
    z_new[...] = alpha * xres_ref[...] + m


def _pool_mixer_kernel(x_ref, xres_ref, w_in_ref, wgrp_ref, bgrp_ref, scale_ref, w_out_ref,
                       g_ref, b_ref, o_ref, u0, u1, z0, z1, gp0, gp1, zc_buf, **static):
    step = pl.program_id(0)

    @pl.when(step == 0)
    def _():
        for ref in (u0, u1, z0, z1):
            ref[...] = jnp.zeros(ref.shape, ref.dtype)

    stages = functools.partial(_pool_stages, step, x_ref, xres_ref, w_in_ref, wgrp_ref, bgrp_ref,
                               scale_ref, w_out_ref, g_ref, b_ref, o_ref)
    parity = lax.rem(step, 2)

    @pl.when(parity == 0)
    def _():
        stages(u0, u1, z0, z1, (gp0, gp1), zc_buf, **static)

    @pl.when(parity == 1)
    def _():
        stages(u1, u0, z1, z0, (gp0, gp1), zc_buf, **static)


def _mlp_ple_kernel(x_ref, p_ref, w1_ref, w2_ref, g_ref, b_ref, wple_ref, wgate_ref, bgate_ref,
                    o_ref, *, alpha):
    x = x_ref[...]
    h = jnp.maximum(_dot(x, w1_ref[...]), 0.0)
    m = _dot(h * h, w2_ref[...])
    x2 = _layer_norm(alpha * x + m, g_ref[...], b_ref[...])
    gate = jax.nn.sigmoid(_dot(x2, wgate_ref[...]) + bgate_ref[...])
    o_ref[...] = x2 + jnp.dot(p_ref[...], wple_ref[...], preferred_element_type=_F32) * gate


def _resident(shape, slot):
    nd = len(shape)
    return pl.BlockSpec((None,) + tuple(shape[1:]), lambda j: (slot,) + (0,) * (nd - 1),
                        pipeline_mode=pl.Buffered(1))


def _rows_spec(rows, width):
    return pl.BlockSpec((rows, width), lambda j: (j, 0))


def _lru_mixer(x2d, slot, layer, w_in, cw, cb, wg, bg, lam, w_out, ln_g, ln_b, *, nb, alpha):
    n_rows, d_model = x2d.shape
    heads, bw = wg.shape[1], wg.shape[2]
    d_rnn = heads * bw
    tb = _MIXER_TIME_BLOCK
    rows = tb * nb
    hist = (_CONV_WIDTH - 1) * nb
    n_blocks = n_rows // rows
    last = n_blocks - 1
    assert heads % _LRU_OUT_CHUNK_HEADS == 0
    kern = functools.partial(_lru_mixer_kernel, tb=tb, nb=nb, heads=heads, bw=bw, alpha=alpha)
    return pl.pallas_call(
        kern,
        grid=(n_blocks + 3,),
        in_specs=[
            pl.BlockSpec((rows, d_model), lambda j: (jnp.minimum(j, last), 0)),
            pl.BlockSpec((rows, d_model), lambda j: (jnp.clip(j - 2, 0, last), 0)),
            _resident(w_in.shape, slot), _resident(cw.shape, slot), _resident(cb.shape, slot),
            _resident(wg.shape, slot), _resident(bg.shape, slot), _resident(lam.shape, slot),
            _resident(w_out.shape, slot), _resident(ln_g.shape, layer), _resident(ln_b.shape, layer),
        ],
        out_specs=pl.BlockSpec((rows, d_model), lambda j: (jnp.maximum(j - 3, 0), 0)),
        out_shape=jax.ShapeDtypeStruct(x2d.shape, _F32),
        scratch_shapes=[
            pltpu.VMEM((hist + rows, d_rnn), _F32), pltpu.VMEM((hist + rows, d_rnn), _F32),
            pltpu.VMEM((rows, d_rnn), _F32), pltpu.VMEM((rows, d_rnn), _F32),
            pltpu.VMEM((rows, d_rnn), _BF16), pltpu.VMEM((rows, d_rnn), _BF16),
            pltpu.VMEM((rows, d_model), _F32), pltpu.VMEM((rows, d_model), _F32),
            pltpu.VMEM((rows, bw), _F32), pltpu.VMEM((rows, bw), _F32),
            pltpu.VMEM((rows, 2 * bw), _F32), pltpu.VMEM((rows, 2 * bw), _F32),
            pltpu.VMEM((nb, d_rnn), _F32),
        ],
        compiler_params=pltpu.CompilerParams(
            dimension_semantics=("arbitrary",), vmem_limit_bytes=_VMEM_LIMIT_BYTES),
        name="lru_mixer",
    )(x2d, x2d, w_in, cw, cb, wg, bg, lam, w_out, ln_g, ln_b)


def _pool_mixer(x2d, slot, layer, w_in, wgrp, bgrp, scale, w_out, ln_g, ln_b, *, nb, alpha):
    n_rows, d_model = x2d.shape
    gw = wgrp.shape[2]
    d_pool = gw * len(_POOL_WINDOWS)
    tb = _MIXER_TIME_BLOCK
    rows = tb * nb
    hist = (max(_POOL_WINDOWS) - 1) * nb
    n_blocks = n_rows // rows
    last = n_blocks - 1
    assert rows >= hist and nb & (nb - 1) == 0 and gw % _V7X_LANES == 0
    kern = functools.partial(_pool_mixer_kernel, tb=tb, nb=nb, gw=gw, alpha=alpha)
    return pl.pallas_call(
        kern,
        grid=(n_blocks + 2,),
        in_specs=[
            pl.BlockSpec((rows, d_model), lambda j: (jnp.minimum(j, last), 0)),
            pl.BlockSpec((rows, d_model), lambda j: (jnp.clip(j - 1, 0, last), 0)),
            _resident(w_in.shape, slot), _resident(wgrp.shape, slot), _resident(bgrp.shape, slot),
            _resident(scale.shape, slot), _resident(w_out.shape, slot),
            _resident(ln_g.shape, layer), _resident(ln_b.shape, layer),
        ],
        out_specs=pl.BlockSpec((rows, d_model), lambda j: (jnp.maximum(j - 2, 0), 0)),
        out_shape=jax.ShapeDtypeStruct(x2d.shape, _F32),
        scratch_shapes=[
            pltpu.VMEM((hist + rows, d_pool), _F32), pltpu.VMEM((hist + rows, d_pool), _F32),
            pltpu.VMEM((rows, d_model), _F32), pltpu.VMEM((rows, d_model), _F32),
            pltpu.VMEM((rows, gw), _F32), pltpu.VMEM((rows, gw), _F32),
            pltpu.VMEM((rows, d_pool), _BF16),
        ],
        compiler_params=pltpu.CompilerParams(
            dimension_semantics=("arbitrary",), vmem_limit_bytes=_VMEM_LIMIT_BYTES),
        name="pool_mixer",
    )(x2d, x2d, w_in, wgrp, bgrp, scale, w_out, ln_g, ln_b)


def _mlp_ple(x2d, p3d, layer, w1, w2, ln_g, ln_b, wple, wgate, bgate, *, alpha):
    n_rows, d_model = x2d.shape
    ple_dim = p3d.shape[2]
    rows = _MLP_ROW_BLOCK
    kern = functools.partial(_mlp_ple_kernel, alpha=alpha)
    return pl.pallas_call(
        kern,
        grid=(n_rows // rows,),
        in_specs=[
            _rows_spec(rows, d_model),
            pl.BlockSpec((None, rows, ple_dim), lambda j: (layer, j, 0)),
            _resident(w1.shape, layer), _resident(w2.shape, layer),
            _resident(ln_g.shape, layer), _resident(ln_b.shape, layer),
            _resident(wple.shape, layer), _resident(wgate.shape, layer), _resident(bgate.shape, layer),
        ],
        out_specs=_rows_spec(rows, d_model),
        out_shape=jax.ShapeDtypeStruct(x2d.shape, _F32),
        compiler_params=pltpu.CompilerParams(
            dimension_semantics=("parallel",), vmem_limit_bytes=_VMEM_LIMIT_BYTES),
        name="mlp_ple",
    )(x2d, p3d, w1, w2, ln_g, ln_b, wple, wgate, bgate)


def kernel(x, p, lru_w_in, lru_conv_w, lru_conv_b, lru_wa, lru_ba, lru_wx, lru_bx, lru_lambda, lru_w_out, pool_w_in, pool_w_grp, pool_b_grp, pool_scale, pool_w_out, ln_mix_g, ln_mix_b, mlp_w1, mlp_w2, ln_mlp_g, ln_mlp_b, ple_w, ple_gate_w, ple_gate_b):
    nb, seq, d_model = x.shape
    depth = p.shape[0]
    n_a, heads, bw, _ = lru_wa.shape
    alpha = float((2 * depth) ** 0.25)

    x2d = x.transpose(1, 0, 2).reshape(seq * nb, d_model)
    p3d = p.transpose(0, 2, 1, 3).reshape(depth, seq * nb, p.shape[-1]).astype(_BF16)

    row = lambda v: v[:, None, :]
    bf = lambda w: w.astype(_BF16)
    wg = bf(jnp.concatenate([lru_wa, lru_wx], axis=-1))
    bg = jnp.concatenate([lru_ba.reshape(n_a, heads, 1, bw), lru_bx.reshape(n_a, heads, 1, bw)], axis=-1)
    lru_w_in = bf(lru_w_in).reshape(n_a, d_model, 2, heads, bw).transpose(0, 1, 3, 2, 4)
    lru_w_in = lru_w_in.reshape(n_a, d_model, 2 * heads * bw)
    lru_w_out = bf(lru_w_out)
    pool_w_in, pool_w_grp, pool_w_out = bf(pool_w_in), bf(pool_w_grp), bf(pool_w_out)
    mlp_w1, mlp_w2 = bf(mlp_w1), bf(mlp_w2)
    ple_w, ple_gate_w = bf(ple_w), bf(ple_gate_w)
    ln_mix_g, ln_mix_b, ln_mlp_g, ln_mlp_b = row(ln_mix_g), row(ln_mix_b), row(ln_mlp_g), row(ln_mlp_b)
    ple_gate_b = row(ple_gate_b)
    lru_conv_b, lru_lambda = row(lru_conv_b), row(lru_lambda)
    pool_b_grp, pool_scale = row(pool_b_grp), row(pool_scale)

    for i in range(depth):
        slot = i // 2
        if i % 2 == 0:
            x2d = _lru_mixer(x2d, slot, i, lru_w_in, lru_conv_w, lru_conv_b, wg, bg, lru_lambda,
                             lru_w_out, ln_mix_g, ln_mix_b, nb=nb, alpha=alpha)
        else:
            x2d = _pool_mixer(x2d, slot, i, pool_w_in, pool_w_grp, pool_b_grp, pool_scale,
                              pool_w_out, ln_mix_g, ln_mix_b, nb=nb, alpha=alpha)
        x2d = _mlp_ple(x2d, p3d, i, mlp_w1, mlp_w2, ln_mlp_g, ln_mlp_b, ple_w, ple_gate_w,
                       ple_gate_b, alpha=alpha)

    return x2d.reshape(seq, nb, d_model).transpose(1, 0, 2)
```

```python
import functools
import math

import jax
import jax.numpy as jnp
from jax import lax
from jax.experimental import pallas as pl
from jax.experimental.pallas import tpu as pltpu

_LRU_C = 8.0
_CONV_WIDTH = 4
_POOL_WINDOWS = (2, 4, 8, 16)
_LN_EPS = 1e-5

_LOG2_E = math.log2(math.e)
_GELU_C0 = math.sqrt(2.0 / math.pi)
_GELU_C1 = 0.044715 * _GELU_C0
_SQRT_FLOOR = 1e-30

_F32 = jnp.float32
_BF16 = jnp.bfloat16

_V7X_VMEM_BYTES = 64 * 1024 * 1024
_V7X_LANES = 128
_VMEM_LIMIT_BYTES = _V7X_VMEM_BYTES - 8 * 1024 * 1024

_MIXER_TIME_BLOCK = 64
_LRU_OUT_CHUNK_HEADS = 3
_LRU_CHUNK_STEPS = 8
_MLP_ROW_BLOCK = 512


def _layer_norm(z, g, b):
    mu = jnp.mean(z, axis=-1, keepdims=True)
    zc = z - mu
    var = jnp.mean(zc * zc, axis=-1, keepdims=True)
    return zc * lax.rsqrt(var + _LN_EPS) * g + b


def _dot(a, w):
    return jnp.dot(a.astype(_BF16), w, preferred_element_type=_F32)


def _sigmoid(v):
    return 0.5 * jnp.tanh(0.5 * v) + 0.5


def _gelu_tanh(v):
    inner = v * (_GELU_C0 + _GELU_C1 * (v * v))
    return (0.5 * v) * (1.0 + jnp.tanh(inner))


def _lru_stages(step, x_ref, xres_ref, w_in_ref, cw_ref, cb_ref, wg_ref, bg_ref, lam_ref, w_out_ref,
                g_ref, b_ref, o_ref, u_new, y_new, u_cur, y_cur, hg_new, hg_old, z_new, z_old,
                uc_buf, gate_buf, h_ref, x_tm=None, xres_tm=None, *, tb, nb, heads, bw, alpha):
    rows = tb * nb
    hist = (_CONV_WIDTH - 1) * nb
    d_rnn = heads * bw
    d_model = o_ref.shape[1]
    out_chunk = d_model * _LRU_OUT_CHUNK_HEADS // heads
    ln_heads = [h for h in range(heads) if h % _LRU_OUT_CHUNK_HEADS != _LRU_OUT_CHUNK_HEADS - 1]
    ln_rows = rows // len(ln_heads)
    chunk = _LRU_CHUNK_STEPS * nb

    u_new[0:hist, :] = u_cur[rows:rows + hist, :]

    if x_tm is None:
        xb = x_ref[...].astype(_BF16)
        xres_cols = lambda oc: xres_ref[:, oc]
    else:
        _batch_to_time_major(x_ref, x_tm)
        _batch_to_time_major(xres_ref, xres_tm)
        xb = _lane_tiles(x_tm).astype(_BF16)
        xres_cols = lambda oc: jnp.concatenate(
            [xres_tm[k] for k in range(oc.start // _V7X_LANES, oc.stop // _V7X_LANES)], axis=1)
    decay = (-_LRU_C * _LOG2_E) * jax.nn.softplus(-lam_ref[...])
    first_block = step == 1

    def conv_and_gates(h):
        cs = slice(h * bw, (h + 1) * bw)
        uc = jnp.broadcast_to(cb_ref[:, cs], (rows, bw))
        for k in range(_CONV_WIDTH):
            uc = uc + u_cur[k * nb:k * nb + rows, cs] * cw_ref[k:k + 1, cs]
        uc_buf[h % 2][...] = uc
        gate_buf[h % 2][...] = _dot(uc, wg_ref[h]) + bg_ref[h]

    conv_and_gates(0)
    for h in range(heads):
        cs = slice(h * bw, (h + 1) * bw)
        w_uy = jnp.concatenate([w_in_ref[:, cs], w_in_ref[:, d_rnn + h * bw:d_rnn + (h + 1) * bw]], axis=1)
        proj = jnp.dot(xb, w_uy, preferred_element_type=_F32)
        u_new[hist:hist + rows, cs] = proj[:, :bw]
        y_new[:, cs] = proj[:, bw:]

        if h + 1 < heads:
            conv_and_gates(h + 1)

        if h % _LRU_OUT_CHUNK_HEADS == _LRU_OUT_CHUNK_HEADS - 1:
            oc = slice(h // _LRU_OUT_CHUNK_HEADS * out_chunk, (h // _LRU_OUT_CHUNK_HEADS + 1) * out_chunk)
            m = jnp.dot(hg_old[...], w_out_ref[:, oc], preferred_element_type=_F32)
            z_new[:, oc] = alpha * xres_cols(oc) + m
        else:
            rs = slice(ln_heads.index(h) * ln_rows, (ln_heads.index(h) + 1) * ln_rows)
            o_ref[rs, :] = _layer_norm(z_old[rs, :], g_ref[...], b_ref[...])

        hprev = jnp.where(step <= 1, 0.0, h_ref[:, cs])
        for c in range(rows // chunk):
            rsl = slice(c * chunk, (c + 1) * chunk)
            gates = gate_buf[h % 2][rsl, :]
            r = _sigmoid(gates[:, :bw])
            gated = _sigmoid(gates[:, bw:]) * uc_buf[h % 2][rsl, :]
            a = jnp.exp2(r * decay[:, cs])
            v = 1.0 - a * a
            bterm = (v * lax.rsqrt(jnp.maximum(v, _SQRT_FLOOR))) * gated
            hs = []
            for t in range(chunk // nb):
                sl = slice(t * nb, (t + 1) * nb)
                b_t = bterm[sl]
                if c == 0 and t == 0:
                    b_t = jnp.where(first_block, gated[sl], b_t)
                hprev = a[sl] * hprev + b_t
                hs.append(hprev)
            hg = jnp.concatenate(hs, axis=0) * _gelu_tanh(y_cur[rsl, cs])
            hg_new[rsl, cs] = hg.astype(_BF16)
        h_ref[:, cs] = hprev


def _lru_mixer_kernel(x_ref, xres_ref, w_in_ref, cw_ref, cb_ref, wg_ref, bg_ref, lam_ref, w_out_ref,
                      g_ref, b_ref, o_ref, u0, u1, y0, y1, hg0, hg1, z0, z1, uc0, uc1, gt0, gt1, h_ref,
                      *maybe_tm, **static):
    step = pl.program_id(0)

    @pl.when(step == 0)
    def _():
        for ref in (u0, u1, y0, y1, hg0, hg1, z0, z1, h_ref):
            ref[...] = jnp.zeros(ref.shape, ref.dtype)

    stages = functools.partial(_lru_stages, step, x_ref, xres_ref, w_in_ref, cw_ref, cb_ref, wg_ref,
                               bg_ref, lam_ref, w_out_ref, g_ref, b_ref, o_ref)
    parity = lax.rem(step, 2)

    @pl.when(parity == 0)
    def _():
        stages(u0, y0, u1, y1, hg1, hg0, z0, z1, (uc0, uc1), (gt0, gt1), h_ref, *maybe_tm, **static)

    @pl.when(parity == 1)
    def _():
        stages(u1, y1, u0, y0, hg0, hg1, z1, z0, (uc0, uc1), (gt0, gt1), h_ref, *maybe_tm, **static)


def _pool_stages(step, x_ref, xres_ref, w_in_ref, wgrp_ref, bgrp_ref, scale_ref, w_out_ref,
                 g_ref, b_ref, o_ref, u_new, u_cur, z_new, z_old, grp_buf, zc_buf,
                 *, tb, nb, gw, alpha):
    rows = tb * nb
    lanes = _V7X_LANES
    hist = (max(_POOL_WINDOWS) - 1) * nb
    groups = len(_POOL_WINDOWS)

    u_new[0:hist, :] = u_cur[rows:rows + hist, :]

    xb = x_ref[...].astype(_BF16)
    t_idx = (step - 1) * tb + lax.shift_right_logical(
        lax.broadcasted_iota(jnp.int32, (rows, lanes), 0), nb.bit_length() - 1)
    t_idx = jnp.maximum(t_idx, 0)

    def pool_and_group(g):
        w = _POOL_WINDOWS[g]
        cs = slice(g * gw, (g + 1) * gw)
        s = u_cur[hist - (w - 1) * nb:hist + rows, cs]
        d = 1
        while d < w:
            n = s.shape[0]
            s = s[d * nb:, :] + s[:n - d * nb, :]
            d *= 2
        inv_cnt = 1.0 / jnp.minimum(t_idx + 1, w).astype(_F32)
        inv_cnt = jnp.concatenate([inv_cnt] * (gw // lanes), axis=1)
        z = s * inv_cnt - u_cur[hist:hist + rows, cs]
        grp_buf[g % 2][...] = _dot(z, wgrp_ref[g])

    pool_and_group(0)
    for g in range(groups):
        cs = slice(g * gw, (g + 1) * gw)
        u_new[hist:hist + rows, cs] = jnp.dot(xb, w_in_ref[:, cs], preferred_element_type=_F32)
        if g + 1 < groups:
            pool_and_group(g + 1)
        zc_buf[:, cs] = ((grp_buf[g % 2][...] + bgrp_ref[:, cs]) * scale_ref[:, cs]).astype(_BF16)

    m = jnp.dot(zc_buf[...], w_out_ref[...], preferred_element_type=_F32)
    ln_rows = rows // groups
    for c in range(groups):
        rs = slice(c * ln_rows, (c + 1) * ln_rows)
        o_ref[rs, :] = _layer_norm(z_old[rs, :], g_ref[...], b_ref[...])
    z_new[...] = alpha * xres_ref[...] + m


def _pool_mixer_kernel(x_ref, xres_ref, w_in_ref, wgrp_ref, bgrp_ref, scale_ref, w_out_ref,
                       g_ref, b_ref, o_ref, u0, u1, z0, z1, gp0, gp1, zc_buf, **static):
    step = pl.program_id(0)

    @pl.when(step == 0)
    def _():
        for ref in (u0, u1, z0, z1):
            ref[...] = jnp.zeros(ref.shape, ref.dtype)

    stages = functools.partial(_pool_stages, step, x_ref, xres_ref, w_in_ref, wgrp_ref, bgrp_ref,
                               scale_ref, w_out_ref, g_ref, b_ref, o_ref)
    parity = lax.rem(step, 2)

    @pl.when(parity == 0)
    def _():
        stages(u0, u1, z0, z1, (gp0, gp1), zc_buf, **static)

    @pl.when(parity == 1)
    def _():
        stages(u1, u0, z1, z0, (gp0, gp1), zc_buf, **static)


def _batch_to_time_major(src_ref, dst_ref):
    nb, tb, width = src_ref.shape
    for b in range(nb):
        for k in range(width // _V7X_LANES):
            dst_ref[k, pl.ds(b, tb, stride=nb), :] = src_ref[b, :, k * _V7X_LANES:(k + 1) * _V7X_LANES]


def _time_to_batch_major(src_ref, dst_ref):
    nb, tb, width = dst_ref.shape
    for b in range(nb):
        for k in range(width // _V7X_LANES):
            dst_ref[b, :, k * _V7X_LANES:(k + 1) * _V7X_LANES] = src_ref[k, pl.ds(b, tb, stride=nb), :]


def _lane_tiles(ref):
    return jnp.concatenate([ref[k] for k in range(ref.shape[0])], axis=1)


def _mlp_ple_kernel(x_ref, p_ref, w1_ref, w2_ref, g_ref, b_ref, wple_ref, wgate_ref, bgate_ref,
                    o_ref, p_tm, *maybe_o_tm, alpha):
    _batch_to_time_major(p_ref, p_tm)
    x = x_ref[...]
    h = jnp.maximum(_dot(x, w1_ref[...]), 0.0)
    m = _dot(h * h, w2_ref[...])
    x2 = _layer_norm(alpha * x + m, g_ref[...], b_ref[...])
    gate = jax.nn.sigmoid(_dot(x2, wgate_ref[...]) + bgate_ref[...])
    out = x2 + _dot(_lane_tiles(p_tm), wple_ref[...]) * gate
    if maybe_o_tm:
        (o_tm,) = maybe_o_tm
        for k in range(o_tm.shape[0]):
            o_tm[k] = out[:, k * _V7X_LANES:(k + 1) * _V7X_LANES]
        _time_to_batch_major(o_tm, o_ref)
    else:
        o_ref[...] = out


def _resident(shape, slot):
    nd = len(shape)
    return pl.BlockSpec((None,) + tuple(shape[1:]), lambda j: (slot,) + (0,) * (nd - 1),
                        pipeline_mode=pl.Buffered(1))


def _rows_spec(rows, width):
    return pl.BlockSpec((rows, width), lambda j: (j, 0))


def _lru_mixer(x, slot, layer, w_in, cw, cb, wg, bg, lam, w_out, ln_g, ln_b, *, nb, alpha):
    batch_major_in = x.ndim == 3
    d_model = x.shape[-1]
    n_rows = x.size // d_model
    heads, bw = wg.shape[1], wg.shape[2]
    d_rnn = heads * bw
    tb = _MIXER_TIME_BLOCK
    rows = tb * nb
    hist = (_CONV_WIDTH - 1) * nb
    n_blocks = n_rows // rows
    last = n_blocks - 1
    assert heads % _LRU_OUT_CHUNK_HEADS == 0
    kern = functools.partial(_lru_mixer_kernel, tb=tb, nb=nb, heads=heads, bw=bw, alpha=alpha)
    if batch_major_in:
        x_block = lambda index: pl.BlockSpec((nb, tb, d_model), lambda j: (0, index(j), 0))
        tm_scratch = [pltpu.VMEM((d_model // _V7X_LANES, rows, _V7X_LANES), _F32)] * 2
    else:
        x_block = lambda index: pl.BlockSpec((rows, d_model), lambda j: (index(j), 0))
        tm_scratch = []
    return pl.pallas_call(
        kern,
        grid=(n_blocks + 3,),
        in_specs=[
            x_block(lambda j: jnp.minimum(j, last)),
            x_block(lambda j: jnp.clip(j - 2, 0, last)),
            _resident(w_in.shape, slot), _resident(cw.shape, slot), _resident(cb.shape, slot),
            _resident(wg.shape, slot), _resident(bg.shape, slot), _resident(lam.shape, slot),
            _resident(w_out.shape, slot), _resident(ln_g.shape, layer), _resident(ln_b.shape, layer),
        ],
        out_specs=pl.BlockSpec((rows, d_model), lambda j: (jnp.maximum(j - 3, 0), 0)),
        out_shape=jax.ShapeDtypeStruct((n_rows, d_model), _F32),
        scratch_shapes=[
            pltpu.VMEM((hist + rows, d_rnn), _F32), pltpu.VMEM((hist + rows, d_rnn), _F32),
            pltpu.VMEM((rows, d_rnn), _F32), pltpu.VMEM((rows, d_rnn), _F32),
            pltpu.VMEM((rows, d_rnn), _BF16), pltpu.VMEM((rows, d_rnn), _BF16),
            pltpu.VMEM((rows, d_model), _F32), pltpu.VMEM((rows, d_model), _F32),
            pltpu.VMEM((rows, bw), _F32), pltpu.VMEM((rows, bw), _F32),
            pltpu.VMEM((rows, 2 * bw), _F32), pltpu.VMEM((rows, 2 * bw), _F32),
            pltpu.VMEM((nb, d_rnn), _F32),
        ] + tm_scratch,
        compiler_params=pltpu.CompilerParams(
            dimension_semantics=("arbitrary",), vmem_limit_bytes=_VMEM_LIMIT_BYTES),
        name="lru_mixer",
    )(x, x, w_in, cw, cb, wg, bg, lam, w_out, ln_g, ln_b)


def _pool_mixer(x2d, slot, layer, w_in, wgrp, bgrp, scale, w_out, ln_g, ln_b, *, nb, alpha):
    n_rows, d_model = x2d.shape
    gw = wgrp.shape[2]
    d_pool = gw * len(_POOL_WINDOWS)
    tb = _MIXER_TIME_BLOCK
    rows = tb * nb
    hist = (max(_POOL_WINDOWS) - 1) * nb
    n_blocks = n_rows // rows
    last = n_blocks - 1
    assert rows >= hist and nb & (nb - 1) == 0 and gw % _V7X_LANES == 0
    kern = functools.partial(_pool_mixer_kernel, tb=tb, nb=nb, gw=gw, alpha=alpha)
    return pl.pallas_call(
        kern,
        grid=(n_blocks + 2,),
        in_specs=[
            pl.BlockSpec((rows, d_model), lambda j: (jnp.minimum(j, last), 0)),
            pl.BlockSpec((rows, d_model), lambda j: (jnp.clip(j - 1, 0, last), 0)),
            _resident(w_in.shape, slot), _resident(wgrp.shape, slot), _resident(bgrp.shape, slot),
            _resident(scale.shape, slot), _resident(w_out.shape, slot),
            _resident(ln_g.shape, layer), _resident(ln_b.shape, layer),
        ],
        out_specs=pl.BlockSpec((rows, d_model), lambda j: (jnp.maximum(j - 2, 0), 0)),
        out_shape=jax.ShapeDtypeStruct(x2d.shape, _F32),
        scratch_shapes=[
            pltpu.VMEM((hist + rows, d_pool), _F32), pltpu.VMEM((hist + rows, d_pool), _F32),
            pltpu.VMEM((rows, d_model), _F32), pltpu.VMEM((rows, d_model), _F32),
            pltpu.VMEM((rows, gw), _F32), pltpu.VMEM((rows, gw), _F32),
            pltpu.VMEM((rows, d_pool), _BF16),
        ],
        compiler_params=pltpu.CompilerParams(
            dimension_semantics=("arbitrary",), vmem_limit_bytes=_VMEM_LIMIT_BYTES),
        name="pool_mixer",
    )(x2d, x2d, w_in, wgrp, bgrp, scale, w_out, ln_g, ln_b)


def _mlp_ple(x2d, p, layer, w1, w2, ln_g, ln_b, wple, wgate, bgate, *, alpha, batch_major_out):
    n_rows, d_model = x2d.shape
    _, nb, seq, ple_dim = p.shape
    rows = _MLP_ROW_BLOCK
    tb = rows // nb
    kern = functools.partial(_mlp_ple_kernel, alpha=alpha)
    scratch = [pltpu.VMEM((ple_dim // _V7X_LANES, rows, _V7X_LANES), _F32)]
    if batch_major_out:
        out_specs = pl.BlockSpec((nb, tb, d_model), lambda j: (0, j, 0))
        out_shape = jax.ShapeDtypeStruct((nb, seq, d_model), _F32)
        scratch.append(pltpu.VMEM((d_model // _V7X_LANES, rows, _V7X_LANES), _F32))
    else:
        out_specs = _rows_spec(rows, d_model)
        out_shape = jax.ShapeDtypeStruct(x2d.shape, _F32)
    return pl.pallas_call(
        kern,
        grid=(n_rows // rows,),
        in_specs=[
            _rows_spec(rows, d_model),
            pl.BlockSpec((None, nb, tb, ple_dim), lambda j: (layer, 0, j, 0)),
            _resident(w1.shape, layer), _resident(w2.shape, layer),
            _resident(ln_g.shape, layer), _resident(ln_b.shape, layer),
            _resident(wple.shape, layer), _resident(wgate.shape, layer), _resident(bgate.shape, layer),
        ],
        out_specs=out_specs,
        out_shape=out_shape,
        scratch_shapes=scratch,
        compiler_params=pltpu.CompilerParams(
            dimension_semantics=("parallel",), vmem_limit_bytes=_VMEM_LIMIT_BYTES),
        name="mlp_ple",
    )(x2d, p, w1, w2, ln_g, ln_b, wple, wgate, bgate)


def kernel(x, p, lru_w_in, lru_conv_w, lru_conv_b, lru_wa, lru_ba, lru_wx, lru_bx, lru_lambda, lru_w_out, pool_w_in, pool_w_grp, pool_b_grp, pool_scale, pool_w_out, ln_mix_g, ln_mix_b, mlp_w1, mlp_w2, ln_mlp_g, ln_mlp_b, ple_w, ple_gate_w, ple_gate_b):
    nb, seq, d_model = x.shape
    depth = p.shape[0]
    n_a, heads, bw, _ = lru_wa.shape
    alpha = float((2 * depth) ** 0.25)

    x2d = x

    row = lambda v: v[:, None, :]
    bf = lambda w: w.astype(_BF16)
    wg = bf(jnp.concatenate([lru_wa, lru_wx], axis=-1))
    bg = jnp.concatenate([lru_ba.reshape(n_a, heads, 1, bw), lru_bx.reshape(n_a, heads, 1, bw)], axis=-1)
    lru_w_in, lru_w_out = bf(lru_w_in), bf(lru_w_out)
    pool_w_in, pool_w_grp, pool_w_out = bf(pool_w_in), bf(pool_w_grp), bf(pool_w_out)
    mlp_w1, mlp_w2 = bf(mlp_w1), bf(mlp_w2)
    ple_w, ple_gate_w = bf(ple_w), bf(ple_gate_w)
    ln_mix_g, ln_mix_b, ln_mlp_g, ln_mlp_b = row(ln_mix_g), row(ln_mix_b), row(ln_mlp_g), row(ln_mlp_b)
    ple_gate_b = row(ple_gate_b)
    lru_conv_b, lru_lambda = row(lru_conv_b), row(lru_lambda)
    pool_b_grp, pool_scale = row(pool_b_grp), row(pool_scale)

    for i in range(depth):
        slot = i // 2
        if i % 2 == 0:
            x2d = _lru_mixer(x2d, slot, i, lru_w_in, lru_conv_w, lru_conv_b, wg, bg, lru_lambda,
                             lru_w_out, ln_mix_g, ln_mix_b, nb=nb, alpha=alpha)
        else:
            x2d = _pool_mixer(x2d, slot, i, pool_w_in, pool_w_grp, pool_b_grp, pool_scale,
                              pool_w_out, ln_mix_g, ln_mix_b, nb=nb, alpha=alpha)
        x2d = _mlp_ple(x2d, p, i, mlp_w1, mlp_w2, ln_mlp_g, ln_mlp_b, ple_w, ple_gate_w,
                       ple_gate_b, alpha=alpha, batch_major_out=i == depth - 1)

    return x2d
```
